```python
import math
import jax, jax.numpy as jnp
from jax import lax
import numpy as np

D_MODEL = 1024
BATCH = 4
SEQ = 8192
DEPTH = 2

SSM_HEADS = 16
SSM_HEAD_DIM = 64
SSM_INNER = SSM_HEADS * SSM_HEAD_DIM
SSM_GROUPS = 2
SSM_STATE = 128
SSM_CONV = 5
SSM_CHUNK = 128
SSM_XBC = SSM_INNER + 2 * SSM_GROUPS * SSM_STATE
POOL_WIDTH = D_MODEL
POOL_WINDOWS = (2, 4, 8, 16)
POOL_GROUP = POOL_WIDTH // len(POOL_WINDOWS)
EVEN_IN = SSM_INNER + SSM_XBC + 2 * SSM_HEADS + POOL_WIDTH
EVEN_MIX = SSM_INNER + POOL_WIDTH

ATT_HEADS = 8
ATT_HEAD_DIM = 64
ATT_BLOCK = 128
REL_BUCKETS = 32
REL_MAX_DIST = 128

FF = 2816
FF_CONV = 3
EPS = 1e-6

N_EVEN = (DEPTH + 1) // 2
N_ODD = DEPTH // 2

kernel_name = 'hybrid_ssd_pool_diffattn_encoder'


def rmsnorm(x, g):
    xf = x.astype(jnp.float32)
    y = xf * lax.rsqrt(jnp.mean(xf * xf, axis=-1, keepdims=True) + EPS)
    return (y * g.astype(jnp.float32)).astype(x.dtype)


def dwconv(x, w, b):
    width = w.shape[0]
    y = lax.conv_general_dilated(
        x, w[:, None, :].astype(x.dtype), window_strides=(1,),
        padding=[(width // 2, width // 2)],
        dimension_numbers=('NWC', 'WIO', 'NWC'),
        feature_group_count=x.shape[-1])
    return y + b.astype(x.dtype)


def segsum_exp(a):
    cs = jnp.cumsum(a, axis=-1)
    diff = cs[..., :, None] - cs[..., None, :]
    t = a.shape[-1]
    mask = jnp.tril(jnp.ones((t, t), dtype=bool))
    return jnp.exp(jnp.where(mask, diff, -jnp.inf))


def ssd_scan(x, dt, A, Bm, Cm):
    b, l, h, p = x.shape
    g, n = Bm.shape[-2:]
    r = h // g
    L = SSM_CHUNK
    c = l // L
    xd = (x * dt[..., None]).reshape(b, c, L, g, r, p)
    Bc = Bm.reshape(b, c, L, g, n)
    Cc = Cm.reshape(b, c, L, g, n)
    a = (dt * A).reshape(b, c, L, g, r).transpose(0, 3, 4, 1, 2)
    a_cs = jnp.cumsum(a, axis=-1)
    Lmat = segsum_exp(a)
    y_diag = jnp.einsum('bclgn,bcsgn,bgrcls,bcsgrp->bclgrp', Cc, Bc, Lmat, xd)
    decay_states = jnp.exp(a_cs[..., -1:] - a_cs)
    states = jnp.einsum('bclgn,bgrcl,bclgrp->bcgrpn', Bc, decay_states, xd)
    states = jnp.concatenate([jnp.zeros_like(states[:, :1]), states], axis=1)
    decay_chunk = segsum_exp(jnp.pad(a_cs[..., -1], ((0, 0), (0, 0), (0, 0), (1, 0))))
    states = jnp.einsum('bgrzc,bcgrpn->bzgrpn', decay_chunk, states)[:, :-1]
    y_off = jnp.einsum('bclgn,bcgrpn,bgrcl->bclgrp', Cc, states, jnp.exp(a_cs))
    return (y_diag + y_off).reshape(b, l, h, p)


def multiscale_pool(u):
    bsz, s, _ = u.shape
    cs = jnp.pad(jnp.cumsum(u.astype(jnp.float32), axis=1), ((0, 0), (1, 0), (0, 0)))
    pos = jnp.arange(s)
    outs = []
    for gi, w in enumerate(POOL_WINDOWS):
        lo = jnp.clip(pos - w // 2, 0, s)
        hi = jnp.clip(pos - w // 2 + w, 0, s)
        seg = cs[:, :, gi * POOL_GROUP:(gi + 1) * POOL_GROUP]
        cnt = (hi - lo).astype(jnp.float32)
        outs.append((seg[:, hi] - seg[:, lo]) / cnt[None, :, None])
    mean = jnp.concatenate(outs, axis=-1)
    return (mean - u.astype(jnp.float32)).astype(u.dtype)


def even_mixer(h, w_in, conv_w, conv_b, dt_bias, a_log, d_skip, ssm_norm, pool_w, pool_scale, w_out):
    bsz, s, _ = h.shape
    proj = h @ w_in
    z, xbc, dt_raw, u = jnp.split(
        proj, [SSM_INNER, SSM_INNER + SSM_XBC, SSM_INNER + SSM_XBC + 2 * SSM_HEADS], axis=-1)
    xbc = jax.nn.silu(dwconv(xbc, conv_w, conv_b))
    xs, Bm, Cm = jnp.split(xbc, [SSM_INNER, SSM_INNER + SSM_GROUPS * SSM_STATE], axis=-1)
    xs = xs.astype(jnp.float32).reshape(bsz, s, SSM_HEADS, SSM_HEAD_DIM)
    Bm = Bm.astype(jnp.float32).reshape(bsz, s, SSM_GROUPS, SSM_STATE)
    Cm = Cm.astype(jnp.float32).reshape(bsz, s, SSM_GROUPS, SSM_STATE)
    dt = jax.nn.softplus(dt_raw.astype(jnp.float32) + dt_bias.astype(jnp.float32).reshape(-1))
    dt_f, dt_b = dt[..., :SSM_HEADS], dt[..., SSM_HEADS:]
    A = -jnp.exp(a_log.astype(jnp.float32))
    flip = lambda t: jnp.flip(t, axis=1)
    y_f = ssd_scan(xs, dt_f, A[0], Bm, Cm)
    y_b = flip(ssd_scan(flip(xs), flip(dt_b), A[1], flip(Bm), flip(Cm)))
    y = y_f + y_b + xs * d_skip.astype(jnp.float32)[:, None]
    y = y.reshape(bsz, s, SSM_INNER)
    y = rmsnorm(y * jax.nn.silu(z.astype(jnp.float32)), ssm_norm).astype(h.dtype)
    pg = multiscale_pool(u).reshape(bsz, s, len(POOL_WINDOWS), POOL_GROUP)
    yp = jnp.einsum('bsgc,gcd->bsgd', pg, pool_w).reshape(bsz, s, POOL_WIDTH) * pool_scale
    return jnp.concatenate([y, yp.astype(h.dtype)], axis=-1) @ w_out


def t5_bucket(rel):
    half = REL_BUCKETS // 2
    max_exact = half // 2
    ret = (rel > 0).astype(jnp.int32) * half
    n = jnp.abs(rel)
    nf = jnp.maximum(n, 1).astype(jnp.float32)
    large = max_exact + (jnp.log(nf / max_exact) / math.log(REL_MAX_DIST / max_exact)
                         * (half - max_exact)).astype(jnp.int32)
    large = jnp.minimum(large, half - 1)
    return ret + jnp.where(n < max_exact, n, large)


def odd_mixer(h, w_qkv, lam_q1, lam_k1, lam_q2, lam_k2, subln, w_o, rel_table, layer_idx):
    bsz, s, _ = h.shape
    H, dh = ATT_HEADS, ATT_HEAD_DIM
    q, k, v = jnp.split(h @ w_qkv, 3, axis=-1)
    q = q.reshape(bsz, s, H, 2, dh)
    k = k.reshape(bsz, s, H, 2, dh)
    v = v.reshape(bsz, s, H, 2 * dh)
    lam_init = 0.8 - 0.6 * math.exp(-0.3 * layer_idx)
    f32 = jnp.float32
    lam = (jnp.exp(jnp.sum(lam_q1.astype(f32) * lam_k1.astype(f32)))
           - jnp.exp(jnp.sum(lam_q2.astype(f32) * lam_k2.astype(f32))) + lam_init)
    nblk = s // ATT_BLOCK
    qb = jnp.moveaxis(q.reshape(bsz, nblk, ATT_BLOCK, H, 2, dh), 1, 0)
    kpos = jnp.arange(s)
    scale = dh ** -0.5

    def block(args):
        qi, i = args
        qpos = i * ATT_BLOCK + jnp.arange(ATT_BLOCK)
        bias = rel_table[t5_bucket(kpos[None, :] - qpos[:, None])]
        bias = bias.transpose(2, 0, 1).astype(f32)
        sc = jnp.einsum('bqhmd,bkhmd->bhmqk', qi, k).astype(f32) * scale + bias[None, :, None]
        p = jax.nn.softmax(sc, axis=-1)
        a = p[:, :, 0] - lam * p[:, :, 1]
        return jnp.einsum('bhqk,bkhe->bqhe', a.astype(v.dtype), v)

    o = lax.map(block, (qb, jnp.arange(nblk)))
    o = jnp.moveaxis(o, 0, 1).reshape(bsz, s, H, 2 * dh)
    o = rmsnorm(o, subln) * (1.0 - lam_init)
    return o.reshape(bsz, s, H * 2 * dh) @ w_o


def conv_glu(h, w_up, conv_w, conv_b, w_down):
    gate, val = jnp.split(h @ w_up, 2, axis=-1)
    gate = dwconv(gate, conv_w, conv_b)
    return (jax.nn.silu(gate) * val) @ w_down


def setup_inputs(seed: int = 0) -> dict:
    key = jax.random.key(seed)
    ks = iter(jax.random.split(key, 40))
    nrm = lambda shape, sc: jax.random.normal(next(ks), shape, jnp.float32) * sc
    D = D_MODEL
    x = nrm((BATCH, SEQ, D), 1.0)
    c = nrm((BATCH, D), 1.0)
    norm_g = 1.0 + nrm((DEPTH, 4, D), 0.05)
    w_ada = nrm((DEPTH, D, 6 * D), 0.5 * D ** -0.5)
    b_ada = nrm((DEPTH, 6 * D), 0.02)
    e_w_in = nrm((N_EVEN, D, EVEN_IN), D ** -0.5)
    e_conv_w = nrm((N_EVEN, SSM_CONV, SSM_XBC), SSM_CONV ** -0.5)
    e_conv_b = nrm((N_EVEN, SSM_XBC), 0.02)
    dt0 = jnp.exp(jax.random.uniform(next(ks), (N_EVEN, 2, SSM_HEADS), jnp.float32,
                                     math.log(1e-3), math.log(1e-1)))
    e_dt_bias = dt0 + jnp.log(-jnp.expm1(-dt0))
    e_a_log = jnp.log(jax.random.uniform(next(ks), (N_EVEN, 2, SSM_HEADS), jnp.float32, 1.0, 16.0))
    e_d_skip = 1.0 + nrm((N_EVEN, SSM_HEADS), 0.1)
    e_ssm_norm = 1.0 + nrm((N_EVEN, SSM_INNER), 0.05)
    e_pool_w = nrm((N_EVEN, len(POOL_WINDOWS), POOL_GROUP, POOL_GROUP), POOL_GROUP ** -0.5)
    e_pool_scale = 1.0 + nrm((N_EVEN, POOL_WIDTH), 0.1)
    e_w_out = nrm((N_EVEN, EVEN_MIX, D), EVEN_MIX ** -0.5)
    o_w_qkv = nrm((N_ODD, D, 3 * D), D ** -0.5)
    o_lam_q1 = nrm((N_ODD, ATT_HEAD_DIM), 0.1)
    o_lam_k1 = nrm((N_ODD, ATT_HEAD_DIM), 0.1)
    o_lam_q2 = nrm((N_ODD, ATT_HEAD_DIM), 0.1)
    o_lam_k2 = nrm((N_ODD, ATT_HEAD_DIM), 0.1)
    o_subln = 1.0 + nrm((N_ODD, 2 * ATT_HEAD_DIM), 0.05)
    o_w_o = nrm((N_ODD, D, D), D ** -0.5)
    rel_table = nrm((REL_BUCKETS, ATT_HEADS), 0.5)
    f_w_up = nrm((DEPTH, D, 2 * FF), D ** -0.5)
    f_conv_w = nrm((DEPTH, FF_CONV, FF), FF_CONV ** -0.5)
    f_conv_b = nrm((DEPTH, FF), 0.02)
    f_w_down = nrm((DEPTH, FF, D), FF ** -0.5)
    return {'x': x, 'c': c, 'norm_g': norm_g, 'w_ada': w_ada, 'b_ada': b_ada,
            'e_w_in': e_w_in, 'e_conv_w': e_conv_w, 'e_conv_b': e_conv_b,
            'e_dt_bias': e_dt_bias, 'e_a_log': e_a_log, 'e_d_skip': e_d_skip,
            'e_ssm_norm': e_ssm_norm, 'e_pool_w': e_pool_w, 'e_pool_scale': e_pool_scale,
            'e_w_out': e_w_out, 'o_w_qkv': o_w_qkv, 'o_lam_q1': o_lam_q1, 'o_lam_k1': o_lam_k1,
            'o_lam_q2': o_lam_q2, 'o_lam_k2': o_lam_k2, 'o_subln': o_subln, 'o_w_o': o_w_o,
            'rel_table': rel_table, 'f_w_up': f_w_up, 'f_conv_w': f_conv_w,
            'f_conv_b': f_conv_b, 'f_w_down': f_w_down}


def reference(x, c, norm_g, w_ada, b_ada, e_w_in, e_conv_w, e_conv_b, e_dt_bias, e_a_log,
              e_d_skip, e_ssm_norm, e_pool_w, e_pool_scale, e_w_out, o_w_qkv, o_lam_q1,
              o_lam_k1, o_lam_q2, o_lam_k2, o_subln, o_w_o, rel_table, f_w_up, f_conv_w,
              f_conv_b, f_w_down):
    cs = jax.nn.silu(c)
    for i in range(DEPTH):
        mod = (cs @ w_ada[i] + b_ada[i])[:, None, :]
        shift1, scale1, gate1, shift2, scale2, gate2 = jnp.split(mod, 6, axis=-1)
        hm = rmsnorm(x, norm_g[i, 0]) * (1.0 + scale1) + shift1
        if i % 2 == 0:
            e = i // 2
            out = even_mixer(hm, e_w_in[e], e_conv_w[e], e_conv_b[e], e_dt_bias[e], e_a_log[e],
                             e_d_skip[e], e_ssm_norm[e], e_pool_w[e], e_pool_scale[e], e_w_out[e])
        else:
            o = i // 2
            out = odd_mixer(hm, o_w_qkv[o], o_lam_q1[o], o_lam_k1[o], o_lam_q2[o], o_lam_k2[o],
                            o_subln[o], o_w_o[o], rel_table, i)
        x = x + gate1 * rmsnorm(out, norm_g[i, 1])
        hf = rmsnorm(x, norm_g[i, 2]) * (1.0 + scale2) + shift2
        x = x + gate2 * rmsnorm(conv_glu(hf, f_w_up[i], f_conv_w[i], f_conv_b[i], f_w_down[i]),
                                norm_g[i, 3])
    return x
```

```python
import functools
import math

import jax
import jax.numpy as jnp
from jax import lax
from jax.experimental import pallas as pl
from jax.experimental.pallas import tpu as pltpu

F32 = jnp.float32
BF16 = jnp.bfloat16

EPS = 1e-6
SSM_HEADS = 16
SSM_HEAD_DIM = 64
SSM_INNER = SSM_HEADS * SSM_HEAD_DIM
SSM_GROUPS = 2
SSM_STATE = 128
SSM_CONV = 5
SSM_CHUNK = 128
SSM_BC = SSM_GROUPS * SSM_STATE
SSM_XBC = SSM_INNER + 2 * SSM_BC
POOL_WINDOWS = (2, 4, 8, 16)
POOL_GROUP = 256
POOL_WIDTH = POOL_GROUP * len(POOL_WINDOWS)
ATT_HEADS = 8
ATT_HEAD_DIM = 64
ATT_VDIM = 2 * ATT_HEAD_DIM
REL_BUCKETS = 32
REL_MAX_DIST = 128
FF_CONV = 3

LANES = 128
HALO = 16
VMEM_LIMIT = 56 * 1024 * 1024


def _cparams(sem):
    return pltpu.CompilerParams(dimension_semantics=sem, vmem_limit_bytes=VMEM_LIMIT)


def _silu(x):
    return x * (1.0 / (1.0 + jnp.exp(-x)))


def _softplus(x):
    return jnp.maximum(x, 0.0) + jnp.log1p(jnp.exp(-jnp.abs(x)))


def _rms(x, g):
    return x * lax.rsqrt(jnp.mean(x * x, axis=-1, keepdims=True) + EPS) * g


def _dot(a, b):
    return jnp.dot(a, b, preferred_element_type=F32)


def _dot_nt(a, b):
    return lax.dot_general(a, b, (((1,), (1,)), ((), ())), preferred_element_type=F32)


def _split3(a):
    hi = a.astype(BF16)
    r1 = a - hi.astype(F32)
    mid = r1.astype(BF16)
    lo = (r1 - mid.astype(F32)).astype(BF16)
    return hi, mid, lo


def _tile(n, pref):
    t = min(n, pref)
    while n % t:
        t //= 2
    return t


def _ada_kernel(c_ref, w_ref, b_ref, o_ref):
    cs = _silu(c_ref[...])
    w = w_ref[0]
    ch, cm, cl = _split3(cs)
    wh, wm, wl = _split3(w)
    acc = _dot(ch, wh) + (_dot(ch, wm) + _dot(cm, wh)) + (_dot(ch, wl) + _dot(cm, wm) + _dot(cl, wh))
    o_ref[0] = acc + b_ref[0]


def _ada(c, w_ada, b_ada):
    depth, d, n = w_ada.shape
    bsz = c.shape[0]
    rows = 8
    cp = jnp.zeros((rows, d), F32).at[:bsz].set(c)
    tn = 512
    out = pl.pallas_call(
        _ada_kernel,
        grid=(depth, n // tn),
        in_specs=[pl.BlockSpec((rows, d), lambda i, j: (0, 0)),
                  pl.BlockSpec((1, d, tn), lambda i, j: (i, 0, j)),
                  pl.BlockSpec((1, 1, tn), lambda i, j: (i, 0, j))],
        out_specs=pl.BlockSpec((1, rows, tn), lambda i, j: (i, 0, j)),
        out_shape=jax.ShapeDtypeStruct((depth, rows, n), F32),
        compiler_params=_cparams(("arbitrary", "arbitrary")),
        name="ada_mod",
    )(cp, w_ada, b_ada.reshape(depth, 1, n))
    return out[:, :bsz]


def _nmm_kernel(x_ref, g_ref, sc_ref, sh_ref, w_ref, *o_refs, splits, tn):
    h = _rms(x_ref[0], g_ref[...]) * (1.0 + sc_ref[0]) + sh_ref[0]
    hb = h.astype(BF16)
    for (start, width), o_ref in zip(splits, o_refs):
        for c0 in range(0, width, tn):
            cw = min(tn, width - c0)
            o_ref[0, :, c0:c0 + cw] = _dot(hb, w_ref[:, start + c0:start + c0 + cw]).astype(o_ref.dtype)


def _norm_mod_matmul(x, g, scale, shift, w, splits, dtypes, tm, name):
    bsz, s, d = x.shape
    n = w.shape[1]
    tm = _tile(s, tm)
    return pl.pallas_call(
        functools.partial(_nmm_kernel, splits=tuple(splits), tn=512),
        grid=(bsz, s // tm),
        in_specs=[pl.BlockSpec((1, tm, d), lambda b, i: (b, i, 0)),
                  pl.BlockSpec((1, d), lambda b, i: (0, 0)),
                  pl.BlockSpec((1, 1, d), lambda b, i: (b, 0, 0)),
                  pl.BlockSpec((1, 1, d), lambda b, i: (b, 0, 0)),
                  pl.BlockSpec((d, n), lambda b, i: (0, 0))],
        out_specs=[pl.BlockSpec((1, tm, wd), lambda b, i: (b, i, 0)) for _, wd in splits],
        out_shape=[jax.ShapeDtypeStruct((bsz, s, wd), dt) for (_, wd), dt in zip(splits, dtypes)],
        compiler_params=_cparams(("parallel", "arbitrary")),
        name=name,
    )(x, g.reshape(1, d), scale, shift, w)


def _halo_specs(t, c, nt, cblk=0):
    r = t // HALO

    def prev_map(b, i):
        return (b, jnp.maximum(i * r - 1, 0), cblk)

    def next_map(b, i):
        return (b, jnp.minimum((i + 1) * r, nt * r - 1), cblk)

    return (pl.BlockSpec((1, HALO, c), prev_map), pl.BlockSpec((1, HALO, c), next_map))


def _fill_ext(ext_ref, xm_ref, xp_ref, xn_ref, t):
    i = pl.program_id(1)
    last = pl.num_programs(1) - 1
    prev = xp_ref[0].astype(F32)
    nxt = xn_ref[0].astype(F32)
    ext_ref[0:HALO, :] = jnp.where(i == 0, 0.0, prev)
    ext_ref[HALO:HALO + t, :] = xm_ref[0].astype(F32)
    ext_ref[HALO + t:HALO + t + HALO, :] = jnp.where(i == last, 0.0, nxt)


def _conv_from_ext(ext_ref, w_ref, b_ref, width, t, c0, cw):
    pad = width // 2
    acc = b_ref[:, c0:c0 + cw]
    for k in range(width):
        r0 = HALO - pad + k
        acc = acc + w_ref[k:k + 1, c0:c0 + cw] * ext_ref[r0:r0 + t, c0:c0 + cw]
    return acc


def _conv_silu_kernel(xm_ref, xp_ref, xn_ref, w_ref, b_ref, o_ref, ext_ref, *, width, t, tn):
    _fill_ext(ext_ref, xm_ref, xp_ref, xn_ref, t)
    c = o_ref.shape[-1]
    for c0 in range(0, c, tn):
        cw = min(tn, c - c0)
        o_ref[0, :, c0:c0 + cw] = _silu(_conv_from_ext(ext_ref, w_ref, b_ref, width, t, c0, cw)).astype(o_ref.dtype)


def _conv_silu(x, w, b, t, name):
    bsz, s, c = x.shape
    t = _tile(s, t)
    width = w.shape[0]
    prev_spec, next_spec = _halo_specs(t, c, s // t)
    return pl.pallas_call(
        functools.partial(_conv_silu_kernel, width=width, t=t, tn=512),
        grid=(bsz, s // t),
        in_specs=[pl.BlockSpec((1, t, c), lambda b_, i: (b_, i, 0)), prev_spec, next_spec,
                  pl.BlockSpec((width, c), lambda b_, i: (0, 0)),
                  pl.BlockSpec((1, c), lambda b_, i: (0, 0))],
        out_specs=pl.BlockSpec((1, t, c), lambda b_, i: (b_, i, 0)),
        out_shape=jax.ShapeDtypeStruct((bsz, s, c), BF16),
        scratch_shapes=[pltpu.VMEM((t + 2 * HALO, c), F32)],
        compiler_params=_cparams(("parallel", "arbitrary")),
        name=name,
    )(x, x, x, w, b.reshape(1, c))


def _ssd_direction(xbc_ref, dt_ref, h_ref, y_ref, dtb_ref, alog_ref, dskip_ref, *, lane0, reverse):
    L = SSM_CHUNK
    dt = _softplus(dt_ref[0] + dtb_ref[...])
    a = dt * (-jnp.exp(alog_ref[...]))
    row = lax.broadcasted_iota(jnp.int32, (L, L), 0)
    col = lax.broadcasted_iota(jnp.int32, (L, L), 1)
    if reverse:
        tri = col >= row
        tri_t = row >= col
    else:
        tri = col <= row
        tri_t = row <= col
    tmat = jnp.where(tri, 1.0, 0.0).astype(BF16)
    tmat_t = jnp.where(tri_t, 1.0, 0.0).astype(BF16)
    acs = sum(_dot(tmat, p) for p in _split3(a))
    a_t = a.T
    dt_t = dt.T
    acs_t = sum(_dot(p, tmat_t) for p in _split3(a_t))
    e_idx = 0 if reverse else L - 1
    tot_col = acs_t[:, e_idx:e_idx + 1]
    tot_row = acs[e_idx:e_idx + 1, :]
    w_t = dt_t * jnp.exp(tot_col - acs_t)
    dec_row = jnp.exp(tot_row)

    lane = lax.broadcasted_iota(jnp.int32, (1, LANES), 1)
    m_lo_f = jnp.where(lane < SSM_HEAD_DIM, 1.0, 0.0)
    m_hi_f = 1.0 - m_lo_f
    hpg = SSM_HEADS // SSM_GROUPS
    for g in range(SSM_GROUPS):
        b_g = xbc_ref[0, :, SSM_INNER + g * SSM_STATE:SSM_INNER + (g + 1) * SSM_STATE]
        c_g = xbc_ref[0, :, SSM_INNER + SSM_BC + g * SSM_STATE:SSM_INNER + SSM_BC + (g + 1) * SSM_STATE]
        gmat = _dot_nt(c_g, b_g)
        c_gf = c_g.astype(F32)
        b_gt = b_g.astype(F32).T
        for j in range(hpg // 2):
            h0 = g * hpg + 2 * j
            ps = slice(h0 * SSM_HEAD_DIM, (h0 + 2) * SSM_HEAD_DIM)
            xs_pair = xbc_ref[0, :, ps]
            h_pair = h_ref[:, ps]
            y_pair = jnp.zeros((L, LANES), F32)
            s_pair = jnp.zeros((SSM_STATE, LANES), F32)
            dec_pair = jnp.zeros((1, LANES), F32)
            for k, mask_f in ((0, m_lo_f), (1, m_hi_f)):
                hl = lane0 + h0 + k
                colb = acs[:, hl:hl + 1]
                rowb = acs_t[hl:hl + 1, :]
                lmat = jnp.exp(jnp.where(tri, colb - rowb, -jnp.inf))
                mmat = lmat * gmat * dt_t[hl:hl + 1, :]
                chat = c_gf * jnp.exp(colb)
                lhs = jnp.concatenate([mmat, chat], axis=1).astype(BF16)
                xs_m = xs_pair * mask_f.astype(BF16)
                h_m = (h_pair * mask_f).astype(BF16)
                rhs = jnp.concatenate([xs_m, h_m], axis=0)
                y_pair = y_pair + _dot(lhs, rhs)
                btw = (b_gt * w_t[hl:hl + 1, :]).astype(BF16)
                s_pair = s_pair + _dot(btw, xs_m)
                dec_pair = dec_pair + mask_f * dec_row[:, hl:hl + 1]
            h_ref[:, ps] = dec_pair * h_pair + s_pair
            if dskip_ref is not None:
                y_pair = y_pair + xs_pair.astype(F32) * dskip_ref[:, ps]
            y_ref[0, :, ps] = y_pair


def _ssd_kernel(xf_ref, xb_ref, dtf_ref, dtb_ref, bias_ref, alog_ref, dskip_ref, yf_ref, yb_ref, hf_ref, hb_ref):
    @pl.when(pl.program_id(1) == 0)
    def _():
        hf_ref[...] = jnp.zeros_like(hf_ref)
        hb_ref[...] = jnp.zeros_like(hb_ref)

    _ssd_direction(xf_ref, dtf_ref, hf_ref, yf_ref, bias_ref, alog_ref, dskip_ref, lane0=0, reverse=False)
    _ssd_direction(xb_ref, dtb_ref, hb_ref, yb_ref, bias_ref, alog_ref, None, lane0=SSM_HEADS, reverse=True)


def _ssd(xbc_c, dt_raw, dt_bias, a_log, d_skip):
    bsz, s, _ = xbc_c.shape
    L = SSM_CHUNK
    nc = s // L
    pad = LANES - 2 * SSM_HEADS
    bias = jnp.pad(dt_bias.reshape(1, -1).astype(F32), ((0, 0), (0, pad)))
    alog = jnp.pad(a_log.reshape(1, -1).astype(F32), ((0, 0), (0, pad)))
    dsk = jnp.repeat(d_skip.astype(F32), SSM_HEAD_DIM).reshape(1, SSM_INNER)
    fwd = lambda b, c: (b, c, 0)
    bwd = lambda b, c: (b, nc - 1 - c, 0)
    const = lambda b, c: (0, 0)
    return pl.pallas_call(
        _ssd_kernel,
        grid=(bsz, nc),
        in_specs=[pl.BlockSpec((1, L, SSM_XBC), fwd), pl.BlockSpec((1, L, SSM_XBC), bwd),
                  pl.BlockSpec((1, L, LANES), fwd), pl.BlockSpec((1, L, LANES), bwd),
                  pl.BlockSpec((1, LANES), const), pl.BlockSpec((1, LANES), const),
                  pl.BlockSpec((1, SSM_INNER), const)],
        out_specs=[pl.BlockSpec((1, L, SSM_INNER), fwd), pl.BlockSpec((1, L, SSM_INNER), bwd)],
        out_shape=[jax.ShapeDtypeStruct((bsz, s, SSM_INNER), F32)] * 2,
        scratch_shapes=[pltpu.VMEM((SSM_STATE, SSM_INNER), F32)] * 2,
        compiler_params=_cparams(("parallel", "arbitrary")),
        name="ssd_scan",
    )(xbc_c, xbc_c, dt_raw, dt_raw, bias, alog, dsk)


def _pool_kernel(um_ref, up_ref, un_ref, w_ref, sc_ref, o_ref, ext_ref, *, t, seq):
    _fill_ext(ext_ref, um_ref, up_ref, un_ref, t)
    pos = pl.program_id(1) * t + lax.broadcasted_iota(jnp.int32, (t, POOL_GROUP), 0)
    for gi, w in enumerate(POOL_WINDOWS):
        cs = slice(gi * POOL_GROUP, (gi + 1) * POOL_GROUP)
        lo = pos - w // 2
        cnt = (jnp.minimum(lo + w, seq) - jnp.maximum(lo, 0)).astype(F32)
        acc = jnp.zeros((t, POOL_GROUP), F32)
        for k in range(w):
            r0 = HALO - w // 2 + k
            acc = acc + ext_ref[r0:r0 + t, cs]
        pooled = acc / cnt - ext_ref[HALO:HALO + t, cs]
        o_ref[0, :, cs] = (_dot(pooled.astype(BF16), w_ref[gi]) * sc_ref[:, cs]).astype(o_ref.dtype)


def _pool(u, pool_w, pool_scale, t):
    bsz, s, c = u.shape
    t = _tile(s, t)
    prev_spec, next_spec = _halo_specs(t, c, s // t)
    ng = len(POOL_WINDOWS)
    return pl.pallas_call(
        functools.partial(_pool_kernel, t=t, seq=s),
        grid=(bsz, s // t),
        in_specs=[pl.BlockSpec((1, t, c), lambda b, i: (b, i, 0)), prev_spec, next_spec,
                  pl.BlockSpec((ng, POOL_GROUP, POOL_GROUP), lambda b, i: (0, 0, 0)),
                  pl.BlockSpec((1, c), lambda b, i: (0, 0))],
        out_specs=pl.BlockSpec((1, t, c), lambda b, i: (b, i, 0)),
        out_shape=jax.ShapeDtypeStruct((bsz, s, c), BF16),
        scratch_shapes=[pltpu.VMEM((t + 2 * HALO, c), F32)],
        compiler_params=_cparams(("parallel", "arbitrary")),
        name="pool_mix",
    )(u, u, u, pool_w.astype(BF16), pool_scale.reshape(1, c).astype(F32))


def _residual_out(out, x_ref, g_ref, gate_ref, o_ref):
    o_ref[0] = x_ref[0] + gate_ref[0] * _rms(out, g_ref[...])


def _even_out_kernel(yf_ref, yb_ref, z_ref, yp_ref, nrm_ref, w_ref, x_ref, g_ref, gate_ref, o_ref):
    y = (yf_ref[0] + yb_ref[0]) * _silu(z_ref[0].astype(F32))
    yn = _rms(y, nrm_ref[...]).astype(BF16)
    out = _dot(yn, w_ref[0:SSM_INNER, :]) + _dot(yp_ref[0], w_ref[SSM_INNER:, :])
    _residual_out(out, x_ref, g_ref, gate_ref, o_ref)


def _even_out(yf, yb, z, yp, ssm_norm, w_out, x, g, gate, tm):
    bsz, s, d = x.shape
    tm = _tile(s, tm)
    k = w_out.shape[0]
    row = lambda b, i: (b, i, 0)
    const = lambda b, i: (0, 0)
    return pl.pallas_call(
        _even_out_kernel,
        grid=(bsz, s // tm),
        in_specs=[pl.BlockSpec((1, tm, SSM_INNER), row), pl.BlockSpec((1, tm, SSM_INNER), row),
                  pl.BlockSpec((1, tm, SSM_INNER), row), pl.BlockSpec((1, tm, POOL_WIDTH), row),
                  pl.BlockSpec((1, SSM_INNER), const), pl.BlockSpec((k, d), const),
                  pl.BlockSpec((1, tm, d), row), pl.BlockSpec((1, d), const),
                  pl.BlockSpec((1, 1, d), lambda b, i: (b, 0, 0))],
        out_specs=pl.BlockSpec((1, tm, d), row),
        out_shape=jax.ShapeDtypeStruct((bsz, s, d), F32),
        compiler_params=_cparams(("parallel", "arbitrary")),
        name="even_out",
    )(yf, yb, z, yp, ssm_norm.reshape(1, -1).astype(F32), w_out, x, g.reshape(1, d), gate)


def _attn_kernel(q_ref, k_ref, v_ref, bias_ref, lq1_ref, lk1_ref, lq2_ref, lk2_ref, sub_ref, o_ref,
                 m_ref, l_ref, acc_ref, *, t, nk, scale, lam_init):
    qi = pl.program_id(2)
    lane = lax.broadcasted_iota(jnp.int32, (1, ATT_VDIM), 1)
    q = q_ref[0] * jnp.asarray(scale, BF16)
    zero = jnp.zeros_like(q)
    qm = (jnp.where(lane < ATT_HEAD_DIM, q, zero), jnp.where(lane >= ATT_HEAD_DIM, q, zero))
    m_ref[...] = jnp.full_like(m_ref, -jnp.inf)
    l_ref[...] = jnp.zeros_like(l_ref)
    acc_ref[...] = jnp.zeros_like(acc_ref)

    def body(j, carry):
        r0 = pl.multiple_of(j * t, t)
        kb = k_ref[0, pl.ds(r0, t), :]
        vb = v_ref[0, pl.ds(r0, t), :]
        bias = bias_ref[0, jnp.clip(j - qi, -2, 2) + 2]
        for mi in range(2):
            s = _dot_nt(qm[mi], kb) + bias
            m_prev = m_ref[mi]
            m_next = jnp.maximum(m_prev, jnp.max(s, axis=1, keepdims=True))
            alpha = jnp.exp(m_prev - m_next)
            p = jnp.exp(s - m_next[:, 0:1])
            l_ref[mi] = alpha * l_ref[mi] + jnp.sum(p, axis=1, keepdims=True)
            acc_ref[mi] = alpha * acc_ref[mi] + _dot(p.astype(BF16), vb)
            m_ref[mi] = m_next
        return carry

    lax.fori_loop(0, nk, body, 0)

    lam = (jnp.exp(jnp.sum(lq1_ref[...] * lk1_ref[...], axis=-1, keepdims=True))
           - jnp.exp(jnp.sum(lq2_ref[...] * lk2_ref[...], axis=-1, keepdims=True)) + lam_init)
    o = acc_ref[0] / l_ref[0] - lam * (acc_ref[1] / l_ref[1])
    o_ref[0] = (_rms(o, sub_ref[...]) * (1.0 - lam_init)).astype(o_ref.dtype)


def _rel_bucket(rel):
    half = REL_BUCKETS // 2
    max_exact = half // 2
    ret = (rel > 0).astype(jnp.int32) * half
    n = jnp.abs(rel)
    nf = jnp.maximum(n, 1).astype(F32)
    large = max_exact + (jnp.log(nf / max_exact) / math.log(REL_MAX_DIST / max_exact)
                         * (half - max_exact)).astype(jnp.int32)
    large = jnp.minimum(large, half - 1)
    return ret + jnp.where(n < max_exact, n, large)


def _bias_tiles(rel_table, t):
    assert t >= REL_MAX_DIST
    d = jnp.arange(-2, 3)[:, None, None]
    rel = d * t + jnp.arange(t)[None, None, :] - jnp.arange(t)[None, :, None]
    return rel_table.astype(F32)[_rel_bucket(rel)].transpose(3, 0, 1, 2)


def _attention(qkv, rel_table, lq1, lk1, lq2, lk2, subln, layer_idx, t):
    bsz, s, _ = qkv.shape
    t = _tile(s, t)
    nk = s // t
    lam_init = 0.8 - 0.6 * math.exp(-0.3 * layer_idx)
    bias = _bias_tiles(rel_table, t)
    vec = lambda a: a.reshape(1, -1).astype(F32)
    const = lambda b, h, i: (0, 0)
    return pl.pallas_call(
        functools.partial(_attn_kernel, t=t, nk=nk, scale=ATT_HEAD_DIM ** -0.5, lam_init=lam_init),
        grid=(bsz, ATT_HEADS, s // t),
        in_specs=[pl.BlockSpec((1, t, ATT_VDIM), lambda b, h, i: (b, i, h)),
                  pl.BlockSpec((1, s, ATT_VDIM), lambda b, h, i: (b, 0, ATT_HEADS + h)),
                  pl.BlockSpec((1, s, ATT_VDIM), lambda b, h, i: (b, 0, 2 * ATT_HEADS + h)),
                  pl.BlockSpec((1, 5, t, t), lambda b, h, i: (h, 0, 0, 0)),
                  pl.BlockSpec((1, ATT_HEAD_DIM), const), pl.BlockSpec((1, ATT_HEAD_DIM), const),
                  pl.BlockSpec((1, ATT_HEAD_DIM), const), pl.BlockSpec((1, ATT_HEAD_DIM), const),
                  pl.BlockSpec((1, ATT_VDIM), const)],
        out_specs=pl.BlockSpec((1, t, ATT_VDIM), lambda b, h, i: (b, i, h)),
        out_shape=jax.ShapeDtypeStruct((bsz, s, ATT_HEADS * ATT_VDIM), BF16),
        scratch_shapes=[pltpu.VMEM((2, t, 1), F32), pltpu.VMEM((2, t, 1), F32),
                        pltpu.VMEM((2, t, ATT_VDIM), F32)],
        compiler_params=_cparams(("parallel", "parallel", "arbitrary")),
        name="diff_attn",
    )(qkv, qkv, qkv, bias, vec(lq1), vec(lk1), vec(lq2), vec(lk2), vec(subln))


def _proj_res_kernel(a_ref, w_ref, x_ref, g_ref, gate_ref, o_ref):
    _residual_out(_dot(a_ref[0], w_ref[...]), x_ref, g_ref, gate_ref, o_ref)


def _proj_res(a, w, x, g, gate, tm):
    bsz, s, d = x.shape
    k = a.shape[-1]
    tm = _tile(s, tm)
    row = lambda b, i: (b, i, 0)
    const = lambda b, i: (0, 0)
    return pl.pallas_call(
        _proj_res_kernel,
        grid=(bsz, s // tm),
        in_specs=[pl.BlockSpec((1, tm, k), row), pl.BlockSpec((k, d), const),
                  pl.BlockSpec((1, tm, d), row), pl.BlockSpec((1, d), const),
                  pl.BlockSpec((1, 1, d), lambda b, i: (b, 0, 0))],
        out_specs=pl.BlockSpec((1, tm, d), row),
        out_shape=jax.ShapeDtypeStruct((bsz, s, d), F32),
        compiler_params=_cparams(("parallel", "arbitrary")),
        name="proj_res",
    )(a, w, x, g.reshape(1, d), gate)


def _ffn_down_kernel(gm_ref, gp_ref, gn_ref, val_ref, cw_ref, cb_ref, w_ref, x_ref, g_ref, gate_ref, o_ref,
                     ext_ref, h_ref, *, t, tn):
    _fill_ext(ext_ref, gm_ref, gp_ref, gn_ref, t)
    ff = h_ref.shape[-1]
    for c0 in range(0, ff, tn):
        cw = min(tn, ff - c0)
        gc = _conv_from_ext(ext_ref, cw_ref, cb_ref, FF_CONV, t, c0, cw)
        h_ref[:, c0:c0 + cw] = (_silu(gc) * val_ref[0, :, c0:c0 + cw].astype(F32)).astype(BF16)
    _residual_out(_dot(h_ref[...], w_ref[...]), x_ref, g_ref, gate_ref, o_ref)


def _ffn_down(up, conv_w, conv_b, w_down, x, g, gate, t):
    bsz, s, d = x.shape
    ff = w_down.shape[0]
    t = _tile(s, t)
    prev_spec, next_spec = _halo_specs(t, ff, s // t)
    row = lambda b, i: (b, i, 0)
    const = lambda b, i: (0, 0)
    return pl.pallas_call(
        functools.partial(_ffn_down_kernel, t=t, tn=256),
        grid=(bsz, s // t),
        in_specs=[pl.BlockSpec((1, t, ff), row), prev_spec, next_spec,
                  pl.BlockSpec((1, t, ff), lambda b, i: (b, i, 1)),
                  pl.BlockSpec((FF_CONV, ff), const), pl.BlockSpec((1, ff), const),
                  pl.BlockSpec((ff, d), const),
                  pl.BlockSpec((1, t, d), row), pl.BlockSpec((1, d), const),
                  pl.BlockSpec((1, 1, d), lambda b, i: (b, 0, 0))],
        out_specs=pl.BlockSpec((1, t, d), row),
        out_shape=jax.ShapeDtypeStruct((bsz, s, d), F32),
        scratch_shapes=[pltpu.VMEM((t + 2 * HALO, ff), F32), pltpu.VMEM((t, ff), BF16)],
        compiler_params=_cparams(("parallel", "arbitrary")),
        name="ffn_down",
    )(up, up, up, up, conv_w.astype(F32), conv_b.reshape(1, ff).astype(F32), w_down, x, g.reshape(1, d), gate)


def _even_layer(x, g_in, g_out, scale, shift, gate, w_in, conv_w, conv_b, dt_bias, a_log, d_skip, ssm_norm,
                pool_w, pool_scale, w_out):
    d = x.shape[-1]
    o_xbc = SSM_INNER
    o_dt = SSM_INNER + SSM_XBC
    o_u = o_dt + 2 * SSM_HEADS
    w_cat = jnp.concatenate([w_in[:, :o_dt], w_in[:, o_u:], w_in[:, o_dt:o_u],
                             jnp.zeros((d, LANES - 2 * SSM_HEADS), w_in.dtype)], axis=1).astype(BF16)
    splits = [(0, SSM_INNER), (o_xbc, SSM_XBC), (o_dt, POOL_WIDTH), (o_dt + POOL_WIDTH, LANES)]
    z, xbc, u, dt_raw = _norm_mod_matmul(x, g_in, scale, shift, w_cat, splits, [BF16, BF16, BF16, F32], 512,
                                         "even_in_proj")
    xbc_c = _conv_silu(xbc, conv_w.astype(F32), conv_b.astype(F32), 256, "ssm_conv")
    yf, yb = _ssd(xbc_c, dt_raw, dt_bias, a_log, d_skip)
    yp = _pool(u, pool_w, pool_scale, 256)
    return _even_out(yf, yb, z, yp, ssm_norm, w_out.astype(BF16), x, g_out, gate, 256)


def _odd_layer(x, g_in, g_out, scale, shift, gate, w_qkv, lq1, lk1, lq2, lk2, subln, w_o, rel_table, layer_idx):
    n = w_qkv.shape[1]
    (qkv,) = _norm_mod_matmul(x, g_in, scale, shift, w_qkv.astype(BF16), [(0, n)], [BF16], 512, "qkv_proj")
    o = _attention(qkv, rel_table, lq1, lk1, lq2, lk2, subln, layer_idx, 512)
    return _proj_res(o, w_o.astype(BF16), x, g_out, gate, 512)


def _ffn(x, g_in, g_out, scale, shift, gate, w_up, conv_w, conv_b, w_down):
    n = w_up.shape[1]
    (up,) = _norm_mod_matmul(x, g_in, scale, shift, w_up.astype(BF16), [(0, n)], [BF16], 512, "ffn_up")
    return _ffn_down(up, conv_w, conv_b, w_down.astype(BF16), x, g_out, gate, 256)


def kernel(x, c, norm_g, w_ada, b_ada, e_w_in, e_conv_w, e_conv_b, e_dt_bias, e_a_log, e_d_skip, e_ssm_norm, e_pool_w, e_pool_scale, e_w_out, o_w_qkv, o_lam_q1, o_lam_k1, o_lam_q2, o_lam_k2, o_subln, o_w_o, rel_table, f_w_up, f_conv_w, f_conv_b, f_w_down):
    depth = w_ada.shape[0]
    bsz, _, d = x.shape
    mod = _ada(c, w_ada, b_ada).reshape(depth, bsz, 6, 1, d)
    for i in range(depth):
        shift1, scale1, gate1, shift2, scale2, gate2 = (mod[i, :, k] for k in range(6))
        if i % 2 == 0:
            e = i // 2
            x = _even_layer(x, norm_g[i, 0], norm_g[i, 1], scale1, shift1, gate1, e_w_in[e], e_conv_w[e],
                            e_conv_b[e], e_dt_bias[e], e_a_log[e], e_d_skip[e], e_ssm_norm[e], e_pool_w[e],
                            e_pool_scale[e], e_w_out[e])
        else:
            o = i // 2
            x = _odd_layer(x, norm_g[i, 0], norm_g[i, 1], scale1, shift1, gate1, o_w_qkv[o], o_lam_q1[o],
                           o_lam_k1[o], o_lam_q2[o], o_lam_k2[o], o_subln[o], o_w_o[o], rel_table, i)
        x = _ffn(x, norm_g[i, 2], norm_g[i, 3], scale2, shift2, gate2, f_w_up[i], f_conv_w[i], f_conv_b[i],
                 f_w_down[i])
    return x
```
